```python
import math
import jax, jax.numpy as jnp
from jax import lax
import numpy as np

D_MODEL = 2048
BATCH = 2
SEQ = 4096
DEPTH = 1

HEAD_DIM = 128
N_Q_HEADS = 16
N_KV_HEADS = 4
GROUP = N_Q_HEADS // N_KV_HEADS
WINDOW = 128
BLK = 128
ROT_DIM = HEAD_DIM // 4
ROPE_THETA = 500000.0
D_RNN = ((4 * D_MODEL // 3 + 255) // 256) * 256
N_RNN_BLOCKS = 16
RNN_BW = D_RNN // N_RNN_BLOCKS
CONV_W = 4
LRU_C = 8.0
D_FF = ((8 * D_MODEL // 3 + 255) // 256) * 256
EPS = 1e-6
NEG = -1e30

Q_W = N_Q_HEADS * HEAD_DIM
KV_W = N_KV_HEADS * HEAD_DIM
SPLITS = np.cumsum([Q_W, KV_W, KV_W, D_RNN, D_RNN, D_MODEL]).tolist()
IN_W = SPLITS[-1] + D_MODEL

kernel_name = "hybrid_swa_sink_rglru_swiglu"


def rms_norm(x, g):
    xf = x.astype(jnp.float32)
    y = xf * lax.rsqrt(jnp.mean(xf * xf, axis=-1, keepdims=True) + EPS)
    return (y * g.astype(jnp.float32)).astype(x.dtype)


def partial_rope(x, cos, sin):
    xf = x.astype(jnp.float32)
    half = ROT_DIM // 2
    x1, x2, rest = xf[..., :half], xf[..., half:ROT_DIM], xf[..., ROT_DIM:]
    out = jnp.concatenate([x1 * cos - x2 * sin, x2 * cos + x1 * sin, rest], axis=-1)
    return out.astype(x.dtype)


def window_attention(q, k, v, sinks):
    B, S = q.shape[0], q.shape[1]
    nb = S // BLK
    qb = q.reshape(B, nb, BLK, N_KV_HEADS, GROUP, HEAD_DIM)

    def with_prev(t):
        tb = t.reshape(B, nb, BLK, N_KV_HEADS, HEAD_DIM)
        prev = jnp.pad(tb, ((0, 0), (1, 0), (0, 0), (0, 0), (0, 0)))[:, :-1]
        return jnp.concatenate([prev, tb], axis=2)

    kw, vw = with_prev(k), with_prev(v)
    scale = 1.0 / math.sqrt(HEAD_DIM)
    s = jnp.einsum('bnqkgd,bnjkd->bnkgqj', qb, kw).astype(jnp.float32) * scale
    qi = jnp.arange(BLK)[:, None]
    kj = jnp.arange(2 * BLK)[None, :]
    rel = qi + BLK - kj
    band = (rel >= 0) & (rel < WINDOW)
    real = (jnp.arange(nb)[:, None, None] > 0) | (kj >= BLK)[None]
    mask = (band[None] & real)[None, :, None, None]
    s = jnp.where(mask, s, NEG)
    sink = jnp.broadcast_to(
        sinks.astype(jnp.float32).reshape(N_KV_HEADS, GROUP)[None, None, :, :, None, None],
        s.shape[:-1] + (1,))
    p = jax.nn.softmax(jnp.concatenate([s, sink], axis=-1), axis=-1)[..., :-1]
    o = jnp.einsum('bnkgqj,bnjkd->bnqkgd', p.astype(v.dtype), vw)
    return o.reshape(B, S, Q_W)


def causal_depthwise_conv(u, w, b):
    S = u.shape[1]
    up = jnp.pad(u, ((0, 0), (CONV_W - 1, 0), (0, 0)))
    y = b
    for tap in range(CONV_W):
        y = y + w[tap] * up[:, tap:tap + S]
    return y


def rg_lru(u, w_r, b_r, w_i, b_i, lam):
    B, S = u.shape[0], u.shape[1]
    uf = u.astype(jnp.float32)
    ub = uf.reshape(B, S, N_RNN_BLOCKS, RNN_BW)
    r = jax.nn.sigmoid(jnp.einsum('bsnc,ncd->bsnd', ub, w_r.astype(jnp.float32)).reshape(B, S, D_RNN)
                       + b_r.astype(jnp.float32))
    i = jax.nn.sigmoid(jnp.einsum('bsnc,ncd->bsnd', ub, w_i.astype(jnp.float32)).reshape(B, S, D_RNN)
                       + b_i.astype(jnp.float32))
    log_a = -LRU_C * r * jax.nn.softplus(-lam.astype(jnp.float32))
    a = jnp.exp(log_a)
    bterm = jnp.sqrt(jnp.maximum(-jnp.expm1(2.0 * log_a), 0.0)) * (i * uf)

    def combine(left, right):
        a1, b1 = left
        a2, b2 = right
        return a1 * a2, a2 * b1 + b2

    _, h = lax.associative_scan(combine, (a, bterm), axis=1)
    return h.astype(u.dtype)


def setup_inputs(seed: int = 0) -> dict:
    key = jax.random.key(seed)
    ks = jax.random.split(key, 24)
    f32 = jnp.float32
    nrm = lambda k, shape, fan: jax.random.normal(k, shape, f32) * (fan ** -0.5)
    L = DEPTH
    x = jax.random.normal(ks[0], (BATCH, SEQ, D_MODEL), f32)
    offs = jax.random.randint(ks[1], (BATCH, 1), 0, 1024, dtype=jnp.int32)
    positions = offs + jnp.arange(SEQ, dtype=jnp.int32)[None, :]
    a0 = jax.random.uniform(ks[2], (L, D_RNN), f32, 0.9, 0.999)
    return {
        "x": x,
        "positions": positions,
        "norm1_g": 1.0 + 0.02 * jax.random.normal(ks[3], (L, D_MODEL), f32),
        "w_in": nrm(ks[4], (L, D_MODEL, IN_W), D_MODEL),
        "b_gates": 0.02 * jax.random.normal(ks[5], (L, 2 * D_MODEL), f32),
        "q_norm_g": 1.0 + 0.02 * jax.random.normal(ks[6], (L, HEAD_DIM), f32),
        "k_norm_g": 1.0 + 0.02 * jax.random.normal(ks[7], (L, HEAD_DIM), f32),
        "sinks": 0.5 * jax.random.normal(ks[8], (L, N_Q_HEADS), f32),
        "conv_w": nrm(ks[9], (L, CONV_W, D_RNN), CONV_W),
        "conv_b": 0.02 * jax.random.normal(ks[10], (L, D_RNN), f32),
        "w_rgate": nrm(ks[11], (L, N_RNN_BLOCKS, RNN_BW, RNN_BW), RNN_BW),
        "b_rgate": 0.02 * jax.random.normal(ks[12], (L, D_RNN), f32),
        "w_igate": nrm(ks[13], (L, N_RNN_BLOCKS, RNN_BW, RNN_BW), RNN_BW),
        "b_igate": 0.02 * jax.random.normal(ks[14], (L, D_RNN), f32),
        "lru_lambda": jnp.log(a0) - jnp.log1p(-a0),
        "w_attn_proj": nrm(ks[15], (L, Q_W, D_MODEL), Q_W),
        "w_lru_proj": nrm(ks[16], (L, D_RNN, D_MODEL), D_RNN),
        "w_out": nrm(ks[17], (L, D_MODEL, D_MODEL), D_MODEL),
        "norm2_g": 1.0 + 0.02 * jax.random.normal(ks[18], (L, D_MODEL), f32),
        "w_ffn_gate": nrm(ks[19], (L, D_MODEL, D_FF), D_MODEL),
        "w_ffn_up": nrm(ks[20], (L, D_MODEL, D_FF), D_MODEL),
        "w_ffn_down": nrm(ks[21], (L, D_FF, D_MODEL), D_FF),
    }


def reference(x, positions, norm1_g, w_in, b_gates, q_norm_g, k_norm_g, sinks,
              conv_w, conv_b, w_rgate, b_rgate, w_igate, b_igate, lru_lambda,
              w_attn_proj, w_lru_proj, w_out, norm2_g, w_ffn_gate, w_ffn_up, w_ffn_down):
    B, S = x.shape[0], x.shape[1]
    inv_freq = ROPE_THETA ** (-jnp.arange(0, ROT_DIM, 2, dtype=jnp.float32) / ROT_DIM)
    ang = positions.astype(jnp.float32)[..., None] * inv_freq
    cos, sin = jnp.cos(ang)[:, :, None, :], jnp.sin(ang)[:, :, None, :]

    h = x
    for l in range(DEPTH):
        xn = rms_norm(h, norm1_g[l])
        z = xn @ w_in[l]
        q, k, v, u, gr, ga, gl = jnp.split(z, SPLITS, axis=-1)
        g_attn = jax.nn.sigmoid(ga + b_gates[l][:D_MODEL])
        g_lru = jax.nn.sigmoid(gl + b_gates[l][D_MODEL:])

        q = rms_norm(q.reshape(B, S, N_Q_HEADS, HEAD_DIM), q_norm_g[l])
        k = rms_norm(k.reshape(B, S, N_KV_HEADS, HEAD_DIM), k_norm_g[l])
        q = partial_rope(q, cos, sin)
        k = partial_rope(k, cos, sin)
        v = v.reshape(B, S, N_KV_HEADS, HEAD_DIM)
        attn = window_attention(q, k, v, sinks[l])

        uc = causal_depthwise_conv(u, conv_w[l], conv_b[l])
        rec = rg_lru(uc, w_rgate[l], b_rgate[l], w_igate[l], b_igate[l], lru_lambda[l])
        rec = rec * jax.nn.gelu(gr)

        merged = g_attn * (attn @ w_attn_proj[l]) + g_lru * (rec @ w_lru_proj[l])
        h = h + merged @ w_out[l]

        hn = rms_norm(h, norm2_g[l])
        ff = (jax.nn.silu(hn @ w_ffn_gate[l]) * (hn @ w_ffn_up[l])) @ w_ffn_down[l]
        h = h + ff
    return h
```

```python
import functools
import math

import jax
import jax.numpy as jnp
import numpy as np
from jax import lax
from jax.experimental import pallas as pl
from jax.experimental.pallas import tpu as pltpu

D_MODEL = 2048
HEAD_DIM = 128
N_Q_HEADS = 16
N_KV_HEADS = 4
GROUP = N_Q_HEADS // N_KV_HEADS
WINDOW = 128
BLK = 128
ROT_DIM = HEAD_DIM // 4
ROPE_THETA = 500000.0
D_RNN = 2816
N_RNN_BLOCKS = 16
RNN_BW = D_RNN // N_RNN_BLOCKS
CONV_W = 4
LRU_C = 8.0
D_FF = 5632
EPS = 1e-6
NEG = -1e30
Q_W = N_Q_HEADS * HEAD_DIM
KV_W = N_KV_HEADS * HEAD_DIM

LANES = 128
SUBLANES = 8
MXU_N = 256
VMEM_LIMIT = 56 * 1024 * 1024

GATE_GROUPS = D_RNN // MXU_N
GATE_WIN = 640
LRU_TS = 256

BF16 = jnp.bfloat16
F32 = jnp.float32


def _cparams(sem):
    return pltpu.CompilerParams(dimension_semantics=sem, vmem_limit_bytes=VMEM_LIMIT)


def _rmsnorm_kernel(x_ref, g_ref, o_ref):
    x = x_ref[...]
    y = x * lax.rsqrt(jnp.mean(x * x, axis=-1, keepdims=True) + EPS)
    o_ref[...] = (y * g_ref[...]).astype(o_ref.dtype)


def _rmsnorm(x, g, tm=512):
    m, d = x.shape
    return pl.pallas_call(
        _rmsnorm_kernel,
        grid=(m // tm,),
        in_specs=[pl.BlockSpec((tm, d), lambda i: (i, 0)),
                  pl.BlockSpec((1, d), lambda i: (0, 0))],
        out_specs=pl.BlockSpec((tm, d), lambda i: (i, 0)),
        out_shape=jax.ShapeDtypeStruct((m, d), BF16),
        compiler_params=_cparams(("parallel",)),
        name="rmsnorm",
    )(x, g.reshape(1, d))


def _mm_kernel(*refs, dots, n_x, n_w, n_e, epilogue):
    x_refs = refs[:n_x]
    w_refs = refs[n_x:n_x + n_w]
    e_refs = refs[n_x + n_w:n_x + n_w + n_e]
    o_refs = refs[n_x + n_w + n_e:]
    zs = [jnp.dot(x_refs[i][...], w_refs[j][...], preferred_element_type=F32)
          for i, j in dots]
    outs = epilogue(zs, e_refs)
    for o_ref, v in zip(o_refs, outs):
        o_ref[...] = v.astype(o_ref.dtype)


def _mm(xs, ws, dots, extras, epilogue, out_dtypes, tm, tn, name):
    m = xs[0].shape[0]
    n = ws[0].shape[1]
    grid = (n // tn, m // tm)
    in_specs = []
    for x in xs:
        in_specs.append(pl.BlockSpec((tm, x.shape[1]), lambda j, i: (i, 0)))
    for w in ws:
        in_specs.append(pl.BlockSpec((w.shape[0], tn), lambda j, i: (0, j)))
    e_arrays = []
    for arr, kind, off in extras:
        e_arrays.append(arr)
        if kind == "mn":
            in_specs.append(pl.BlockSpec((tm, tn), lambda j, i, off=off: (i, j + off)))
        elif kind == "n":
            in_specs.append(pl.BlockSpec((1, tn), lambda j, i: (0, j)))
        else:
            in_specs.append(pl.BlockSpec((tm, arr.shape[1]), lambda j, i: (i, 0)))
    out_specs = [pl.BlockSpec((tm, tn), lambda j, i: (i, j)) for _ in out_dtypes]
    out_shape = [jax.ShapeDtypeStruct((m, n), dt) for dt in out_dtypes]
    kern = functools.partial(_mm_kernel, dots=dots, n_x=len(xs), n_w=len(ws),
                             n_e=len(extras), epilogue=epilogue)
    outs = pl.pallas_call(
        kern, grid=grid, in_specs=in_specs, out_specs=out_specs, out_shape=out_shape,
        compiler_params=_cparams(("parallel", "parallel")), name=name,
    )(*xs, *ws, *e_arrays)
    return outs


def _ep_plain(zs, e_refs):
    return [zs[0]]


def _ep_gelu(zs, e_refs):
    return [jax.nn.gelu(zs[0])]


def _ep_sigmoid_bias(zs, e_refs):
    return [jax.nn.sigmoid(zs[0] + e_refs[0][...])]


def _ep_qk(zs, e_refs):
    z = zs[0]
    gain_ref, c_ref, s1_ref, s2_ref = e_refs
    c, s1, s2 = c_ref[...], s1_ref[...], s2_ref[...]
    heads = []
    for h in range(z.shape[1] // HEAD_DIM):
        zh = z[:, h * HEAD_DIM:(h + 1) * HEAD_DIM]
        zh = zh * lax.rsqrt(jnp.mean(zh * zh, axis=-1, keepdims=True) + EPS)
        zh = zh * gain_ref[:, h * HEAD_DIM:(h + 1) * HEAD_DIM]
        up = pltpu.roll(zh, HEAD_DIM - ROT_DIM // 2, axis=1)
        down = pltpu.roll(zh, ROT_DIM // 2, axis=1)
        heads.append((zh * c + up * s1 + down * s2).astype(BF16))
    return [jnp.concatenate(heads, axis=1)]


def _ep_merge(zs, e_refs):
    return [e_refs[0][...].astype(F32) * zs[0] + e_refs[1][...].astype(F32) * zs[1]]


def _ep_residual(zs, e_refs):
    return [e_refs[0][...] + zs[0]]


def _ep_swiglu(zs, e_refs):
    return [jax.nn.silu(zs[0]) * zs[1]]


def _attn_kernel(sink_ref, q_ref, kc_ref, kp_ref, vc_ref, vp_ref, o_ref):
    nblk = pl.program_id(1)
    qi = lax.broadcasted_iota(jnp.int32, (BLK, 2 * BLK), 0)
    kj = lax.broadcasted_iota(jnp.int32, (BLK, 2 * BLK), 1)
    rel = qi + BLK - kj
    mask = (rel >= 0) & (rel < WINDOW) & ((nblk > 0) | (kj >= BLK))
    for kv in range(N_KV_HEADS):
        ksl = slice(kv * HEAD_DIM, (kv + 1) * HEAD_DIM)
        kcat = jnp.concatenate([kp_ref[:, ksl], kc_ref[:, ksl]], axis=0)
        vcat = jnp.concatenate([vp_ref[:, ksl], vc_ref[:, ksl]], axis=0)
        for g in range(GROUP):
            h = kv * GROUP + g
            hsl = slice(h * HEAD_DIM, (h + 1) * HEAD_DIM)
            s = lax.dot_general(q_ref[:, hsl], kcat, (((1,), (1,)), ((), ())),
                                preferred_element_type=F32)
            s = jnp.where(mask, s, NEG)
            sink = sink_ref[h]
            mx = jnp.maximum(jnp.max(s, axis=-1, keepdims=True), sink)
            p = jnp.exp(s - mx)
            denom = jnp.sum(p, axis=-1, keepdims=True) + jnp.exp(sink - mx)
            o = jnp.dot(p.astype(BF16), vcat, preferred_element_type=F32)
            o_ref[:, hsl] = (o / denom).astype(o_ref.dtype)


def _attention(qk, v, sinks, batch, seq):
    nb = seq // BLK
    k_col = Q_W // KV_W
    cur = lambda b, n: (b * nb + n, 0)
    prev = lambda b, n: (b * nb + jnp.maximum(n - 1, 0), 0)
    cur_k = lambda b, n: (b * nb + n, k_col)
    prev_k = lambda b, n: (b * nb + jnp.maximum(n - 1, 0), k_col)
    return pl.pallas_call(
        _attn_kernel,
        grid=(batch, nb),
        in_specs=[pl.BlockSpec(memory_space=pltpu.SMEM),
                  pl.BlockSpec((BLK, Q_W), cur),
                  pl.BlockSpec((BLK, KV_W), cur_k),
                  pl.BlockSpec((BLK, KV_W), prev_k),
                  pl.BlockSpec((BLK, KV_W), cur),
                  pl.BlockSpec((BLK, KV_W), prev)],
        out_specs=pl.BlockSpec((BLK, Q_W), cur),
        out_shape=jax.ShapeDtypeStruct((batch * seq, Q_W), BF16),
        compiler_params=_cparams(("parallel", "parallel")),
        name="window_attention",
    )(sinks, qk, qk, qk, v, v)


def _gate_windows():
    plan = []
    for p in range(GATE_GROUPS):
        c0, c1 = p * MXU_N, (p + 1) * MXU_N
        b0, b1 = c0 // RNN_BW, (c1 - 1) // RNN_BW
        start = min((b0 * RNN_BW) // LANES, (D_RNN - GATE_WIN) // LANES)
        assert (b1 + 1) * RNN_BW - start * LANES <= GATE_WIN
        pieces = []
        for b in range(b0, b1 + 1):
            lo, hi = max(b * RNN_BW, c0), min((b + 1) * RNN_BW, c1)
            pieces.append((b, b * RNN_BW - start * LANES, lo - b * RNN_BW, hi - b * RNN_BW, lo - c0))
        plan.append((start, pieces))
    return plan


_GATE_PLAN = _gate_windows()


def _window_weights(w):
    groups = []
    for start, pieces in _GATE_PLAN:
        wp = jnp.zeros((GATE_WIN, MXU_N), w.dtype)
        for b, row_off, col_lo, col_hi, dst in pieces:
            wp = wp.at[row_off:row_off + RNN_BW, dst:dst + (col_hi - col_lo)].set(w[b][:, col_lo:col_hi])
        groups.append(wp)
    return jnp.stack(groups).astype(BF16)


def _lru_kernel(u_ref, gg_ref, cw_ref, cb_ref, wr_ref, wi_ref, br_ref, bi_ref, lam_ref,
                o_ref, ubuf, uc_s, a_s, b_s, h_s):
    ts = u_ref.shape[0]

    @pl.when(pl.program_id(1) == 0)
    def _():
        ubuf[0:SUBLANES, :] = jnp.zeros((SUBLANES, D_RNN), F32)
        h_s[...] = jnp.zeros_like(h_s)

    ubuf[SUBLANES:, :] = u_ref[...].astype(F32)
    uc = cb_ref[...] + cw_ref[0:1, :] * ubuf[pl.ds(SUBLANES - 3, ts), :]
    for tap in range(1, CONV_W):
        uc = uc + cw_ref[tap:tap + 1, :] * ubuf[pl.ds(SUBLANES - 3 + tap, ts), :]
    uc_s[...] = uc
    ubuf[0:SUBLANES, :] = ubuf[ts:ts + SUBLANES, :]

    lam = lam_ref[...]
    neg_softplus = -(jnp.maximum(-lam, 0.0) + jnp.log1p(jnp.exp(-jnp.abs(lam))))
    for p, (start, _) in enumerate(_GATE_PLAN):
        win = uc_s[:, start * LANES:start * LANES + GATE_WIN].astype(BF16)
        csl = slice(p * MXU_N, (p + 1) * MXU_N)
        r = jax.nn.sigmoid(jnp.dot(win, wr_ref[p], preferred_element_type=F32) + br_ref[:, csl])
        i = jax.nn.sigmoid(jnp.dot(win, wi_ref[p], preferred_element_type=F32) + bi_ref[:, csl])
        log_a = LRU_C * r * neg_softplus[:, csl]
        a = jnp.exp(log_a)
        one_minus_a2 = -jnp.tanh(log_a) * (a * a + 1.0)
        a_s[:, csl] = a
        b_s[:, csl] = jnp.sqrt(jnp.maximum(one_minus_a2, 0.0)) * (i * uc_s[:, csl])

    row = lax.broadcasted_iota(jnp.int32, (SUBLANES, D_RNN), 0)

    def chunk(c, h_prev):
        r0 = pl.multiple_of(c * SUBLANES, SUBLANES)
        a = a_s[pl.ds(r0, SUBLANES), :]
        b = b_s[pl.ds(r0, SUBLANES), :]
        for d in (1, 2, 4):
            keep = row >= d
            b = a * jnp.where(keep, pltpu.roll(b, d, axis=0), 0.0) + b
            a = a * jnp.where(keep, pltpu.roll(a, d, axis=0), 1.0)
        h = b + a * h_prev
        b_s[pl.ds(r0, SUBLANES), :] = h
        return jnp.broadcast_to(h[SUBLANES - 1:SUBLANES, :], (SUBLANES, D_RNN))

    h_s[...] = lax.fori_loop(0, ts // SUBLANES, chunk, h_s[...])
    o_ref[...] = (b_s[...] * gg_ref[...].astype(F32)).astype(o_ref.dtype)


def _rg_lru(u, gg, conv_w, conv_b, w_r, b_r, w_i, b_i, lam, batch, seq):
    ts = LRU_TS
    nt = seq // ts
    row = lambda b, t: (b * nt + t, 0)
    full2 = lambda b, t: (0, 0)
    full3 = lambda b, t: (0, 0, 0)
    vec = lambda a: a.reshape(1, D_RNN)
    return pl.pallas_call(
        _lru_kernel,
        grid=(batch, nt),
        in_specs=[pl.BlockSpec((ts, D_RNN), row),
                  pl.BlockSpec((ts, D_RNN), row),
                  pl.BlockSpec((CONV_W, D_RNN), full2),
                  pl.BlockSpec((1, D_RNN), full2),
                  pl.BlockSpec((GATE_GROUPS, GATE_WIN, MXU_N), full3),
                  pl.BlockSpec((GATE_GROUPS, GATE_WIN, MXU_N), full3),
                  pl.BlockSpec((1, D_RNN), full2),
                  pl.BlockSpec((1, D_RNN), full2),
                  pl.BlockSpec((1, D_RNN), full2)],
        out_specs=pl.BlockSpec((ts, D_RNN), row),
        out_shape=jax.ShapeDtypeStruct((batch * seq, D_RNN), BF16),
        scratch_shapes=[pltpu.VMEM((ts + SUBLANES, D_RNN), F32),
                        pltpu.VMEM((ts, D_RNN), F32),
                        pltpu.VMEM((ts, D_RNN), F32),
                        pltpu.VMEM((ts, D_RNN), F32),
                        pltpu.VMEM((SUBLANES, D_RNN), F32)],
        compiler_params=_cparams(("parallel", "arbitrary")),
        name="conv_rg_lru",
    )(u, gg, conv_w, vec(conv_b), _window_weights(w_r), _window_weights(w_i),
      vec(b_r), vec(b_i), vec(lam))


def _rope_tables(positions):
    half = ROT_DIM // 2
    inv_freq = ROPE_THETA ** (-jnp.arange(0, ROT_DIM, 2, dtype=F32) / ROT_DIM)
    ang = positions.astype(F32).reshape(-1, 1) * inv_freq
    cos, sin = jnp.cos(ang), jnp.sin(ang)
    t = ang.shape[0]
    pad = jnp.zeros((t, HEAD_DIM - ROT_DIM), F32)
    zero = jnp.zeros((t, half), F32)
    c = jnp.concatenate([cos, cos, pad + 1.0], axis=1)
    s1 = jnp.concatenate([-sin, zero, pad], axis=1)
    s2 = jnp.concatenate([zero, sin, pad], axis=1)
    return c, s1, s2


def kernel(x, positions, norm1_g, w_in, b_gates, q_norm_g, k_norm_g, sinks, conv_w, conv_b,
           w_rgate, b_rgate, w_igate, b_igate, lru_lambda, w_attn_proj, w_lru_proj, w_out,
           norm2_g, w_ffn_gate, w_ffn_up, w_ffn_down):
    batch, seq, _ = x.shape
    t = batch * seq
    h = x.reshape(t, D_MODEL)
    rope_c, rope_s1, rope_s2 = _rope_tables(positions)
    o_k, o_v, o_u, o_gr, o_g = Q_W, Q_W + KV_W, Q_W + 2 * KV_W, Q_W + 2 * KV_W + D_RNN, Q_W + 2 * KV_W + 2 * D_RNN

    for l in range(w_in.shape[0]):
        wb = lambda w: w.astype(BF16)
        w_qk, w_v = wb(w_in[l][:, :o_v]), wb(w_in[l][:, o_v:o_u])
        w_u, w_gr, w_g = wb(w_in[l][:, o_u:o_gr]), wb(w_in[l][:, o_gr:o_g]), wb(w_in[l][:, o_g:])
        qk_gain = jnp.concatenate([jnp.tile(q_norm_g[l] * (1.0 / math.sqrt(HEAD_DIM)), N_Q_HEADS),
                                   jnp.tile(k_norm_g[l], N_KV_HEADS)]).reshape(1, Q_W + KV_W)

        xn = _rmsnorm(h, norm1_g[l])
        (qk,) = _mm([xn], [w_qk], [(0, 0)],
                    [(qk_gain, "n", 0), (rope_c, "m", 0), (rope_s1, "m", 0), (rope_s2, "m", 0)],
                    _ep_qk, [BF16], 1024, 1280, "in_proj_qk")
        (v,) = _mm([xn], [w_v], [(0, 0)], [], _ep_plain, [BF16], 1024, 512, "in_proj_v")
        (u,) = _mm([xn], [w_u], [(0, 0)], [], _ep_plain, [BF16], 1024, 1408, "in_proj_u")
        (gg,) = _mm([xn], [w_gr], [(0, 0)], [], _ep_gelu, [BF16], 1024, 1408, "in_proj_gelu")
        (gates,) = _mm([xn], [w_g], [(0, 0)], [(b_gates[l].reshape(1, -1), "n", 0)],
                       _ep_sigmoid_bias, [BF16], 1024, 1024, "in_proj_gates")

        attn = _attention(qk, v, sinks[l], batch, seq)
        rec = _rg_lru(u, gg, conv_w[l], conv_b[l], w_rgate[l], b_rgate[l], w_igate[l], b_igate[l],
                      lru_lambda[l], batch, seq)

        tn = 1024
        (merged,) = _mm([attn, rec], [wb(w_attn_proj[l]), wb(w_lru_proj[l])], [(0, 0), (1, 1)],
                        [(gates, "mn", 0), (gates, "mn", D_MODEL // tn)],
                        _ep_merge, [BF16], 512, tn, "merge_proj")
        (h,) = _mm([merged], [wb(w_out[l])], [(0, 0)], [(h, "mn", 0)], _ep_residual, [F32],
                   1024, 1024, "out_proj")

        hn = _rmsnorm(h, norm2_g[l])
        (act,) = _mm([hn], [wb(w_ffn_gate[l]), wb(w_ffn_up[l])], [(0, 0), (0, 1)], [],
                     _ep_swiglu, [BF16], 1024, 512, "ffn_gate_up")
        (h,) = _mm([act], [wb(w_ffn_down[l])], [(0, 0)], [(h, "mn", 0)], _ep_residual, [F32],
                   512, 1024, "ffn_down")
    return h.reshape(batch, seq, D_MODEL)
```

```python
import functools
import math

import jax
import jax.numpy as jnp
import numpy as np
from jax import lax
from jax.experimental import pallas as pl
from jax.experimental.pallas import tpu as pltpu

D_MODEL = 2048
HEAD_DIM = 128
N_Q_HEADS = 16
N_KV_HEADS = 4
GROUP = N_Q_HEADS // N_KV_HEADS
WINDOW = 128
BLK = 128
ROT_DIM = HEAD_DIM // 4
ROPE_THETA = 500000.0
D_RNN = 2816
N_RNN_BLOCKS = 16
RNN_BW = D_RNN // N_RNN_BLOCKS
CONV_W = 4
LRU_C = 8.0
D_FF = 5632
EPS = 1e-6
NEG = -1e30
Q_W = N_Q_HEADS * HEAD_DIM
KV_W = N_KV_HEADS * HEAD_DIM
QKV_W = Q_W + 2 * KV_W

LANES = 128
SUBLANES = 8
MXU_N = 256
VMEM_LIMIT = 56 * 1024 * 1024
CAST_ROWS = 256

GATE_GROUPS = D_RNN // MXU_N
GATE_WIN = 640
LRU_TS = 256

BF16 = jnp.bfloat16
F32 = jnp.float32


def _cparams(sem):
    return pltpu.CompilerParams(dimension_semantics=sem, vmem_limit_bytes=VMEM_LIMIT)


def _split_bf16(v):
    hi = v.astype(BF16)
    lo = (v - hi.astype(F32)).astype(BF16)
    return hi, lo


def _rmsnorm_kernel(x_ref, g_ref, o_ref):
    x = x_ref[...]
    y = x * lax.rsqrt(jnp.mean(x * x, axis=-1, keepdims=True) + EPS)
    o_ref[...] = (y * g_ref[...]).astype(o_ref.dtype)


def _rmsnorm(x, g, tm=512):
    m, d = x.shape
    return pl.pallas_call(
        _rmsnorm_kernel,
        grid=(m // tm,),
        in_specs=[pl.BlockSpec((tm, d), lambda i: (i, 0)),
                  pl.BlockSpec((1, d), lambda i: (0, 0))],
        out_specs=pl.BlockSpec((tm, d), lambda i: (i, 0)),
        out_shape=jax.ShapeDtypeStruct((m, d), BF16),
        compiler_params=_cparams(("parallel",)),
        name="rmsnorm",
    )(x, g.reshape(1, d))


def _mm_kernel(*refs, dots, n_x, n_w, n_e, n_o, epilogue):
    x_refs = refs[:n_x]
    w_refs = refs[n_x:n_x + n_w]
    e_refs = refs[n_x + n_w:n_x + n_w + n_e]
    o_refs = refs[n_x + n_w + n_e:n_x + n_w + n_e + n_o]
    wb_refs = refs[n_x + n_w + n_e + n_o:]

    @pl.when(pl.program_id(1) == 0)
    def _():
        for w_ref, wb_ref in zip(w_refs, wb_refs):
            def cast_rows(c, carry, w_ref=w_ref, wb_ref=wb_ref):
                r0 = pl.multiple_of(c * CAST_ROWS, CAST_ROWS)
                wb_ref[pl.ds(r0, CAST_ROWS), :] = w_ref[pl.ds(r0, CAST_ROWS), :].astype(BF16)
                return carry
            lax.fori_loop(0, w_ref.shape[0] // CAST_ROWS, cast_rows, 0)

    zs = [jnp.dot(x_refs[i][...], wb_refs[j][...], preferred_element_type=F32)
          for i, j in dots]
    epilogue(zs, e_refs, o_refs, pl.program_id(0))


def _mm(xs, ws, dots, extras, epilogue, out_dtypes, n_cols, tm, tn, name, single_buffer_w=False):
    m = xs[0].shape[0]
    grid = (n_cols // tn, m // tm)
    w_mode = dict(pipeline_mode=pl.Buffered(1)) if single_buffer_w else {}
    in_specs = []
    for x in xs:
        in_specs.append(pl.BlockSpec((tm, x.shape[1]), lambda j, i: (i, 0)))
    for w, off in ws:
        in_specs.append(pl.BlockSpec((w.shape[0], tn), lambda j, i, off=off: (0, j + off), **w_mode))
    e_arrays = []
    for arr, kind, off in extras:
        e_arrays.append(arr)
        if kind == "mn":
            in_specs.append(pl.BlockSpec((tm, tn), lambda j, i, off=off: (i, j + off)))
        elif kind == "n":
            in_specs.append(pl.BlockSpec((1, tn), lambda j, i: (0, j)))
        elif kind == "m":
            in_specs.append(pl.BlockSpec((tm, arr.shape[1]), lambda j, i: (i, 0)))
        else:
            in_specs.append(pl.BlockSpec(arr.shape, lambda j, i: (0, 0)))
    out_specs = [pl.BlockSpec((tm, tn), lambda j, i: (i, j)) for _ in out_dtypes]
    out_shape = [jax.ShapeDtypeStruct((m, n_cols), dt) for dt in out_dtypes]
    scratch = [pltpu.VMEM((w.shape[0], tn), BF16) for w, _ in ws]
    kern = functools.partial(_mm_kernel, dots=dots, n_x=len(xs), n_w=len(ws),
                             n_e=len(extras), n_o=len(out_dtypes), epilogue=epilogue)
    return pl.pallas_call(
        kern, grid=grid, in_specs=in_specs, out_specs=out_specs, out_shape=out_shape,
        scratch_shapes=scratch,
        compiler_params=_cparams(("arbitrary", "arbitrary")), name=name,
    )(*xs, *[w for w, _ in ws], *e_arrays)


def _ep_qkv(zs, e_refs, o_refs, j):
    z = zs[0]
    gain_ref, c_ref, s_ref, mean_ref, swap_ref = e_refs
    o_ref = o_refs[0]
    heads_per_tile = z.shape[1] // HEAD_DIM
    c, s = c_ref[...], s_ref[...]
    mean_mat, swap_mat = mean_ref[...], swap_ref[...]

    def norm_rope(h):
        sl = slice(h * HEAD_DIM, (h + 1) * HEAD_DIM)
        zh = z[:, sl]
        hi, lo = _split_bf16(zh * zh)
        ms = (jnp.dot(hi, mean_mat, preferred_element_type=F32)
              + jnp.dot(lo, mean_mat, preferred_element_type=F32))
        zg = zh * lax.rsqrt(ms + EPS) * gain_ref[:, sl]
        hi, lo = _split_bf16(zg)
        partner = (jnp.dot(hi, swap_mat, preferred_element_type=F32)
                   + jnp.dot(lo, swap_mat, preferred_element_type=F32))
        o_ref[:, sl] = (zg * c + partner * s).astype(o_ref.dtype)

    for h in range(N_KV_HEADS):
        norm_rope(h)

    @pl.when(j < Q_W // z.shape[1])
    def _():
        for h in range(N_KV_HEADS, heads_per_tile):
            norm_rope(h)

    @pl.when(j == Q_W // z.shape[1])
    def _():
        o_ref[:, KV_W:] = z[:, KV_W:].astype(o_ref.dtype)


def _ep_u_gelu(zs, e_refs, o_refs, j):
    z = zs[0]
    o_ref = o_refs[0]
    tn = z.shape[1]
    split_tile, split_col = D_RNN // tn, D_RNN % tn

    @pl.when(j < split_tile)
    def _():
        o_ref[...] = z.astype(o_ref.dtype)

    @pl.when(j == split_tile)
    def _():
        o_ref[:, :split_col] = z[:, :split_col].astype(o_ref.dtype)
        o_ref[:, split_col:] = jax.nn.gelu(z[:, split_col:]).astype(o_ref.dtype)

    @pl.when(j > split_tile)
    def _():
        o_ref[...] = jax.nn.gelu(z).astype(o_ref.dtype)


def _ep_sigmoid_bias(zs, e_refs, o_refs, j):
    o_refs[0][...] = jax.nn.sigmoid(zs[0] + e_refs[0][...]).astype(o_refs[0].dtype)


def _ep_merge(zs, e_refs, o_refs, j):
    merged = e_refs[0][...].astype(F32) * zs[0] + e_refs[1][...].astype(F32) * zs[1]
    o_refs[0][...] = merged.astype(o_refs[0].dtype)


def _ep_residual(zs, e_refs, o_refs, j):
    o_refs[0][...] = e_refs[0][...] + zs[0]


def _ep_swiglu(zs, e_refs, o_refs, j):
    o_refs[0][...] = (jax.nn.silu(zs[0]) * zs[1]).astype(o_refs[0].dtype)


def _attn_kernel(sink_ref, q_ref, kc_ref, kp_ref, vc_ref, vp_ref, o_ref):
    nblk = pl.program_id(1)
    qi = lax.broadcasted_iota(jnp.int32, (BLK, 2 * BLK), 0)
    kj = lax.broadcasted_iota(jnp.int32, (BLK, 2 * BLK), 1)
    rel = qi + BLK - kj
    mask = (rel >= 0) & (rel < WINDOW) & ((nblk > 0) | (kj >= BLK))
    for kv in range(N_KV_HEADS):
        ksl = slice(kv * HEAD_DIM, (kv + 1) * HEAD_DIM)
        kcat = jnp.concatenate([kp_ref[:, ksl], kc_ref[:, ksl]], axis=0)
        vcat = jnp.concatenate([vp_ref[:, ksl], vc_ref[:, ksl]], axis=0)
        for g in range(GROUP):
            h = kv * GROUP + g
            hsl = slice(h * HEAD_DIM, (h + 1) * HEAD_DIM)
            s = lax.dot_general(q_ref[:, hsl], kcat, (((1,), (1,)), ((), ())),
                                preferred_element_type=F32)
            s = jnp.where(mask, s, NEG)
            sink = sink_ref[h]
            mx = jnp.maximum(jnp.max(s, axis=-1, keepdims=True), sink)
            p = jnp.exp(s - mx)
            denom = jnp.sum(p, axis=-1, keepdims=True) + jnp.exp(sink - mx)
            o = jnp.dot(p.astype(BF16), vcat, preferred_element_type=F32)
            o_ref[:, hsl] = (o / denom).astype(o_ref.dtype)


def _attention(qkv, sinks, batch, seq):
    nb = seq // BLK
    k_col, v_col = Q_W // KV_W, Q_W // KV_W + 1
    q_cur = lambda b, n: (b * nb + n, 0)
    cur = lambda col: (lambda b, n: (b * nb + n, col))
    prev = lambda col: (lambda b, n: (b * nb + jnp.maximum(n - 1, 0), col))
    return pl.pallas_call(
        _attn_kernel,
        grid=(batch, nb),
        in_specs=[pl.BlockSpec(memory_space=pltpu.SMEM),
                  pl.BlockSpec((BLK, Q_W), q_cur),
                  pl.BlockSpec((BLK, KV_W), cur(k_col)),
                  pl.BlockSpec((BLK, KV_W), prev(k_col)),
                  pl.BlockSpec((BLK, KV_W), cur(v_col)),
                  pl.BlockSpec((BLK, KV_W), prev(v_col))],
        out_specs=pl.BlockSpec((BLK, Q_W), q_cur),
        out_shape=jax.ShapeDtypeStruct((batch * seq, Q_W), BF16),
        compiler_params=_cparams(("parallel", "parallel")),
        name="window_attention",
    )(sinks, qkv, qkv, qkv, qkv, qkv)


def _gate_windows():
    plan = []
    for p in range(GATE_GROUPS):
        c0, c1 = p * MXU_N, (p + 1) * MXU_N
        b0, b1 = c0 // RNN_BW, (c1 - 1) // RNN_BW
        start = min((b0 * RNN_BW) // LANES, (D_RNN - GATE_WIN) // LANES)
        assert (b1 + 1) * RNN_BW - start * LANES <= GATE_WIN
        pieces = []
        for b in range(b0, b1 + 1):
            lo, hi = max(b * RNN_BW, c0), min((b + 1) * RNN_BW, c1)
            pieces.append((b, b * RNN_BW - start * LANES, lo - b * RNN_BW, hi - b * RNN_BW, lo - c0))
        plan.append((start, pieces))
    return plan


_GATE_PLAN = _gate_windows()


def _window_weights(w):
    groups = []
    for start, pieces in _GATE_PLAN:
        wp = jnp.zeros((GATE_WIN, MXU_N), w.dtype)
        for b, row_off, col_lo, col_hi, dst in pieces:
            wp = wp.at[row_off:row_off + RNN_BW, dst:dst + (col_hi - col_lo)].set(w[b][:, col_lo:col_hi])
        groups.append(wp)
    return jnp.stack(groups).astype(BF16)


def _lru_kernel(u_ref, gg_ref, cw_ref, cb_ref, wr_ref, wi_ref, br_ref, bi_ref, lam_ref,
                o_ref, ubuf, uc_s, a_s, b_s, h_s):
    ts = u_ref.shape[0]

    @pl.when(pl.program_id(1) == 0)
    def _():
        ubuf[0:SUBLANES, :] = jnp.zeros((SUBLANES, D_RNN), F32)
        h_s[...] = jnp.zeros_like(h_s)

    ubuf[SUBLANES:, :] = u_ref[...].astype(F32)
    uc = cb_ref[...] + cw_ref[0:1, :] * ubuf[pl.ds(SUBLANES - 3, ts), :]
    for tap in range(1, CONV_W):
        uc = uc + cw_ref[tap:tap + 1, :] * ubuf[pl.ds(SUBLANES - 3 + tap, ts), :]
    uc_s[...] = uc
    ubuf[0:SUBLANES, :] = ubuf[ts:ts + SUBLANES, :]

    lam = lam_ref[...]
    neg_softplus = -(jnp.maximum(-lam, 0.0) + jnp.log1p(jnp.exp(-jnp.abs(lam))))
    for p, (start, _) in enumerate(_GATE_PLAN):
        win = uc_s[:, start * LANES:start * LANES + GATE_WIN].astype(BF16)
        csl = slice(p * MXU_N, (p + 1) * MXU_N)
        r = jax.nn.sigmoid(jnp.dot(win, wr_ref[p], preferred_element_type=F32) + br_ref[:, csl])
        i = jax.nn.sigmoid(jnp.dot(win, wi_ref[p], preferred_element_type=F32) + bi_ref[:, csl])
        a = jnp.exp(LRU_C * r * neg_softplus[:, csl])
        a_s[:, csl] = a
        b_s[:, csl] = jnp.sqrt(jnp.maximum(1.0 - a * a, 0.0)) * (i * uc_s[:, csl])

    row = lax.broadcasted_iota(jnp.int32, (SUBLANES, D_RNN), 0)

    def chunk(c, h_prev):
        r0 = pl.multiple_of(c * SUBLANES, SUBLANES)
        a = a_s[pl.ds(r0, SUBLANES), :]
        b = b_s[pl.ds(r0, SUBLANES), :]
        for d in (1, 2, 4):
            keep = row >= d
            b = a * jnp.where(keep, pltpu.roll(b, d, axis=0), 0.0) + b
            a = a * jnp.where(keep, pltpu.roll(a, d, axis=0), 1.0)
        h = b + a * h_prev
        b_s[pl.ds(r0, SUBLANES), :] = h
        return jnp.broadcast_to(h[SUBLANES - 1:SUBLANES, :], (SUBLANES, D_RNN))

    h_s[...] = lax.fori_loop(0, ts // SUBLANES, chunk, h_s[...])
    o_ref[...] = (b_s[...] * gg_ref[...].astype(F32)).astype(o_ref.dtype)


def _rg_lru(ug, conv_w, conv_b, w_r, b_r, w_i, b_i, lam, batch, seq):
    ts = LRU_TS
    nt = seq // ts
    col = lambda c: (lambda b, t: (b * nt + t, c))
    full2 = lambda b, t: (0, 0)
    full3 = lambda b, t: (0, 0, 0)
    vec = lambda a: a.reshape(1, D_RNN)
    return pl.pallas_call(
        _lru_kernel,
        grid=(batch, nt),
        in_specs=[pl.BlockSpec((ts, D_RNN), col(0)),
                  pl.BlockSpec((ts, D_RNN), col(1)),
                  pl.BlockSpec((CONV_W, D_RNN), full2),
                  pl.BlockSpec((1, D_RNN), full2),
                  pl.BlockSpec((GATE_GROUPS, GATE_WIN, MXU_N), full3),
                  pl.BlockSpec((GATE_GROUPS, GATE_WIN, MXU_N), full3),
                  pl.BlockSpec((1, D_RNN), full2),
                  pl.BlockSpec((1, D_RNN), full2),
                  pl.BlockSpec((1, D_RNN), full2)],
        out_specs=pl.BlockSpec((ts, D_RNN), col(0)),
        out_shape=jax.ShapeDtypeStruct((batch * seq, D_RNN), BF16),
        scratch_shapes=[pltpu.VMEM((ts + SUBLANES, D_RNN), F32),
                        pltpu.VMEM((ts, D_RNN), F32),
                        pltpu.VMEM((ts, D_RNN), F32),
                        pltpu.VMEM((ts, D_RNN), F32),
                        pltpu.VMEM((SUBLANES, D_RNN), F32)],
        compiler_params=_cparams(("parallel", "arbitrary")),
        name="conv_rg_lru",
    )(ug, ug, conv_w, vec(conv_b), _window_weights(w_r), _window_weights(w_i),
      vec(b_r), vec(b_i), vec(lam))


def _rope_tables(positions):
    half = ROT_DIM // 2
    inv_freq = ROPE_THETA ** (-jnp.arange(0, ROT_DIM, 2, dtype=F32) / ROT_DIM)
    ang = positions.astype(F32).reshape(-1, 1) * inv_freq
    cos, sin = jnp.cos(ang), jnp.sin(ang)
    pad = jnp.zeros((ang.shape[0], HEAD_DIM - ROT_DIM), F32)
    c = jnp.concatenate([cos, cos, pad + 1.0], axis=1)
    s = jnp.concatenate([-sin, sin, pad], axis=1)
    return c, s


def _head_matrices():
    half = ROT_DIM // 2
    mean_mat = np.full((HEAD_DIM, HEAD_DIM), 1.0 / HEAD_DIM, np.float32)
    swap = np.zeros((HEAD_DIM, HEAD_DIM), np.float32)
    for j in range(half):
        swap[j + half, j] = 1.0
        swap[j, j + half] = 1.0
    return jnp.asarray(mean_mat, BF16), jnp.asarray(swap, BF16)


def kernel(x, positions, norm1_g, w_in, b_gates, q_norm_g, k_norm_g, sinks, conv_w, conv_b,
           w_rgate, b_rgate, w_igate, b_igate, lru_lambda, w_attn_proj, w_lru_proj, w_out,
           norm2_g, w_ffn_gate, w_ffn_up, w_ffn_down):
    batch, seq, _ = x.shape
    t = batch * seq
    h = x.reshape(t, D_MODEL)
    rope_c, rope_s = _rope_tables(positions)
    mean_mat, swap_mat = _head_matrices()

    for l in range(w_in.shape[0]):
        qkv_gain = jnp.concatenate([jnp.tile(q_norm_g[l] * (1.0 / math.sqrt(HEAD_DIM)), N_Q_HEADS),
                                    jnp.tile(k_norm_g[l], N_KV_HEADS),
                                    jnp.ones((KV_W,), F32)]).reshape(1, QKV_W)

        xn = _rmsnorm(h, norm1_g[l])
        (qkv,) = _mm([xn], [(w_in[l], 0)], [(0, 0)],
                     [(qkv_gain, "n", 0), (rope_c, "m", 0), (rope_s, "m", 0),
                      (mean_mat, "full", 0), (swap_mat, "full", 0)],
                     _ep_qkv, [BF16], QKV_W, 1024, 1024, "in_proj_qkv")
        (ug,) = _mm([xn], [(w_in[l], QKV_W // 512)], [(0, 0)], [],
                    _ep_u_gelu, [BF16], 2 * D_RNN, 2048, 512, "in_proj_lru")
        (gates,) = _mm([xn], [(w_in[l], (QKV_W + 2 * D_RNN) // 512)], [(0, 0)],
                       [(b_gates[l].reshape(1, -1), "n", 0)],
                       _ep_sigmoid_bias, [BF16], 2 * D_MODEL, 2048, 512, "in_proj_gates")

        attn = _attention(qkv, sinks[l], batch, seq)
        rec = _rg_lru(ug, conv_w[l], conv_b[l], w_rgate[l], b_rgate[l], w_igate[l], b_igate[l],
                      lru_lambda[l], batch, seq)

        tn = 512
        (merged,) = _mm([attn, rec], [(w_attn_proj[l], 0), (w_lru_proj[l], 0)], [(0, 0), (1, 1)],
                        [(gates, "mn", 0), (gates, "mn", D_MODEL // tn)],
                        _ep_merge, [BF16], D_MODEL, 1024, tn, "merge_proj", single_buffer_w=True)
        (h,) = _mm([merged], [(w_out[l], 0)], [(0, 0)], [(h, "mn", 0)], _ep_residual, [F32],
                   D_MODEL, 1024, 1024, "out_proj")

        hn = _rmsnorm(h, norm2_g[l])
        (act,) = _mm([hn], [(w_ffn_gate[l], 0), (w_ffn_up[l], 0)], [(0, 0), (0, 1)], [],
                     _ep_swiglu, [BF16], D_FF, 1024, 512, "ffn_gate_up")
        (h,) = _mm([act], [(w_ffn_down[l], 0)], [(0, 0)], [(h, "mn", 0)], _ep_residual, [F32],
                   D_MODEL, 1024, 512, "ffn_down", single_buffer_w=True)
    return h.reshape(batch, seq, D_MODEL)
```
